```python
import functools
import jax, jax.numpy as jnp
from jax import lax
import numpy as np

D_MODEL = 1024
BATCH = 2
SEQ = 8192
DEPTH = 2
DEC_BATCH = 128
DEC_SEQ = 1
PAST_LEN = 2048
PAGE_SIZE = 128

D_MIX = D_MODEL
W_A = D_MIX // 4
W_B = D_MIX // 4
W_C = D_MIX // 4
W_D = D_MIX - W_A - W_B - W_C
N_HEADS_D = 4
HEAD_DIM = W_D // N_HEADS_D
CONV_A_WIDTH = 3
CONV_B_WIDTH = 31
POOL_WINDOWS = (2, 4, 8, 16)
N_POOL_GROUPS = len(POOL_WINDOWS)
POOL_GROUP = W_C // N_POOL_GROUPS
POOL_HIST = max(POOL_WINDOWS) - 1
Q_BLOCK = 128
LN_EPS = 1e-5
SB_BIAS_INIT = -5.0
DEEPNORM_ALPHA = (2 * DEPTH) ** 0.25
DEEPNORM_BETA = (8 * DEPTH) ** -0.25
D_IN = 4 * W_A + 3 * W_B + 2 * W_C + 4 * W_D

kernel_name = "hybrid_conv_pool_stickbreak_decoder_step"


def _split_in(u):
    sizes = (W_A,) * 4 + (W_B,) * 3 + (W_C,) * 2 + (W_D,) * 4
    idx = []
    off = 0
    for s in sizes[:-1]:
        off += s
        idx.append(off)
    return jnp.split(u, idx, axis=-1)


def _layernorm(x, g, b):
    xf = x.astype(jnp.float32)
    mu = jnp.mean(xf, axis=-1, keepdims=True)
    var = jnp.mean(jnp.square(xf - mu), axis=-1, keepdims=True)
    return ((xf - mu) * lax.rsqrt(var + LN_EPS) * g + b).astype(x.dtype)


def _dwconv(u_ext, w):
    c = w.shape[1]
    return lax.conv_general_dilated(u_ext, w[:, None, :].astype(u_ext.dtype), window_strides=(1,),
                                    padding='VALID', dimension_numbers=('NWC', 'WIO', 'NWC'),
                                    feature_group_count=c)


def _pool_mix(v_ext, start_pos, pool_w, pool_scale):
    b_, l_ext, _ = v_ext.shape
    l = l_ext - POOL_HIST
    vf = v_ext.astype(jnp.float32)
    cs = jnp.concatenate([jnp.zeros((b_, 1, W_C), jnp.float32), jnp.cumsum(vf, axis=1)], axis=1)
    pos = start_pos + jnp.arange(l)
    outs = []
    for g, w in enumerate(POOL_WINDOWS):
        sl = slice(g * POOL_GROUP, (g + 1) * POOL_GROUP)
        win = cs[:, POOL_HIST + 1:, sl] - cs[:, POOL_HIST + 1 - w:POOL_HIST + 1 - w + l, sl]
        cnt = jnp.minimum(pos + 1, w).astype(jnp.float32)
        outs.append(win / cnt[None, :, None])
    pooled = jnp.stack(outs, axis=2)
    v_new = vf[:, POOL_HIST:, :].reshape(b_, l, N_POOL_GROUPS, POOL_GROUP)
    y = jnp.einsum('blgc,gcd->blgd', pooled - v_new, pool_w.astype(jnp.float32))
    y = y.reshape(b_, l, W_C) * pool_scale.astype(jnp.float32)
    return y.astype(v_ext.dtype)


def _stick_breaking_block(q, k, v, bias, q_pos, k_pos):
    z = jnp.einsum('bqhd,bshd->bhqs', q.astype(jnp.float32), k.astype(jnp.float32)) * (HEAD_DIM ** -0.5)
    z = z + bias.astype(jnp.float32)[None, :, None, None]
    mask = (k_pos[None, :] < q_pos[:, None])[None, None]
    log_keep = jnp.where(mask, jax.nn.log_sigmoid(-z), 0.0)
    later = lax.cumsum(log_keep, axis=3, reverse=True) - log_keep
    a = jnp.where(mask, jnp.exp(jax.nn.log_sigmoid(z) + later), 0.0)
    return jnp.einsum('bhqs,bshd->bqhd', a, v.astype(jnp.float32)).astype(q.dtype)


def _stick_breaking_prompt(q, k, v, bias):
    b_, l, h, dh = q.shape
    nb = l // Q_BLOCK
    qb = q.reshape(b_, nb, Q_BLOCK, h, dh).transpose(1, 0, 2, 3, 4)
    k_pos = jnp.arange(l)

    def body(args):
        qi, bi = args
        q_pos = bi * Q_BLOCK + jnp.arange(Q_BLOCK)
        return _stick_breaking_block(qi, k, v, bias, q_pos, k_pos)

    out = lax.map(body, (qb, jnp.arange(nb)))
    return out.transpose(1, 0, 2, 3, 4).reshape(b_, l, h, dh)


def _stick_breaking_sample(q, k, v, bias, k_past, v_past):
    k_all = jnp.concatenate([k_past.astype(k.dtype), k], axis=1)
    v_all = jnp.concatenate([v_past.astype(v.dtype), v], axis=1)
    q_pos = k_past.shape[1] + jnp.arange(q.shape[1])
    k_pos = jnp.arange(k_all.shape[1])
    return _stick_breaking_block(q, k_all, v_all, bias, q_pos, k_pos)


def _layer(x, hist_a, hist_b, hist_c, attend, start_pos, w_in, conv_a_w, conv_b_w, conv_b_b,
           norm_b_g, norm_b_b, pool_w, pool_scale, w_out, ln_g, ln_b, sb_bias):
    b_, l, _ = x.shape
    u = jnp.einsum('bld,de->ble', x, w_in)
    (a_h, a_b, a_c, a_g, b_a, b_b, b_g, c_v, c_g, d_q, d_k, d_v, d_g) = _split_in(u)
    a_ext = jnp.concatenate([hist_a.astype(u.dtype), a_c * a_h], axis=1)
    y_a = a_b * _dwconv(a_ext, conv_a_w) * jax.nn.silu(a_g)
    b_ext = jnp.concatenate([hist_b.astype(u.dtype), b_a * jax.nn.sigmoid(b_b)], axis=1)
    b_c = _dwconv(b_ext, conv_b_w) + conv_b_b
    y_b = jax.nn.silu(_layernorm(b_c, norm_b_g, norm_b_b)) * jax.nn.silu(b_g)
    c_ext = jnp.concatenate([hist_c.astype(u.dtype), c_v], axis=1)
    y_c = _pool_mix(c_ext, start_pos, pool_w, pool_scale) * jax.nn.silu(c_g)
    q = d_q.reshape(b_, l, N_HEADS_D, HEAD_DIM)
    k = d_k.reshape(b_, l, N_HEADS_D, HEAD_DIM)
    v = d_v.reshape(b_, l, N_HEADS_D, HEAD_DIM)
    y_d = attend(q, k, v, sb_bias).reshape(b_, l, W_D) * jax.nn.silu(d_g)
    mix = jnp.concatenate([y_a, y_b, y_c, y_d], axis=-1)
    y = _layernorm(DEEPNORM_ALPHA * x + jnp.einsum('ble,ed->bld', mix, w_out), ln_g, ln_b)
    return (y, a_ext[:, -(CONV_A_WIDTH - 1):], b_ext[:, -(CONV_B_WIDTH - 1):],
            c_ext[:, -POOL_HIST:], k, v)


def setup_inputs(seed: int = 0) -> dict:
    key = jax.random.key(seed)
    ks = jax.random.split(key, 24)
    n_pages = PAST_LEN // PAGE_SIZE
    n_used = DEC_BATCH * n_pages
    n_pool = n_used + n_used // 4
    f32 = jnp.float32
    nrm = lambda k, shape: jax.random.normal(k, shape, f32)
    page_table = jax.random.permutation(ks[0], n_pool)[:n_used].reshape(DEC_BATCH, n_pages).astype(jnp.int32)
    return {
        'x_prompt': nrm(ks[1], (BATCH, SEQ, D_MODEL)),
        'x_sample': nrm(ks[2], (DEC_BATCH, DEC_SEQ, D_MODEL)),
        'cache_k': nrm(ks[3], (DEPTH, n_pool, PAGE_SIZE, N_HEADS_D, HEAD_DIM)),
        'cache_v': nrm(ks[4], (DEPTH, n_pool, PAGE_SIZE, N_HEADS_D, HEAD_DIM)),
        'page_table': page_table,
        'state_conv_a': nrm(ks[5], (DEPTH, DEC_BATCH, CONV_A_WIDTH - 1, W_A)),
        'state_conv_b': 0.5 * nrm(ks[6], (DEPTH, DEC_BATCH, CONV_B_WIDTH - 1, W_B)),
        'state_pool': nrm(ks[7], (DEPTH, DEC_BATCH, POOL_HIST, W_C)),
        'w_in': nrm(ks[8], (DEPTH, D_MODEL, D_IN)) * D_MODEL ** -0.5,
        'conv_a_w': nrm(ks[9], (DEPTH, CONV_A_WIDTH, W_A)) * CONV_A_WIDTH ** -0.5,
        'conv_b_w': nrm(ks[10], (DEPTH, CONV_B_WIDTH, W_B)) * CONV_B_WIDTH ** -0.5,
        'conv_b_b': 0.02 * nrm(ks[11], (DEPTH, W_B)),
        'norm_b_g': 1.0 + 0.02 * nrm(ks[12], (DEPTH, W_B)),
        'norm_b_b': 0.02 * nrm(ks[13], (DEPTH, W_B)),
        'pool_w': nrm(ks[14], (DEPTH, N_POOL_GROUPS, POOL_GROUP, POOL_GROUP)) * POOL_GROUP ** -0.5,
        'pool_scale': 1.0 + 0.1 * nrm(ks[15], (DEPTH, W_C)),
        'w_out': nrm(ks[16], (DEPTH, D_MIX, D_MODEL)) * (D_MIX ** -0.5 * DEEPNORM_BETA),
        'ln_g': 1.0 + 0.02 * nrm(ks[17], (DEPTH, D_MODEL)),
        'ln_b': 0.02 * nrm(ks[18], (DEPTH, D_MODEL)),
        'sb_bias': SB_BIAS_INIT + 0.1 * nrm(ks[19], (DEPTH, N_HEADS_D)),
    }


def reference(x_prompt, x_sample, cache_k, cache_v, page_table, state_conv_a, state_conv_b, state_pool,
              w_in, conv_a_w, conv_b_w, conv_b_b, norm_b_g, norm_b_b, pool_w, pool_scale, w_out,
              ln_g, ln_b, sb_bias):
    bp = x_prompt.shape[0]
    bs = x_sample.shape[0]
    zero_a = jnp.zeros((bp, CONV_A_WIDTH - 1, W_A), x_prompt.dtype)
    zero_b = jnp.zeros((bp, CONV_B_WIDTH - 1, W_B), x_prompt.dtype)
    zero_c = jnp.zeros((bp, POOL_HIST, W_C), x_prompt.dtype)
    hp, hs = x_prompt, x_sample
    ca_p, ca_s, cb_p, cb_s, pl_p, pl_s, kp_l, vp_l, ks_l, vs_l = ([] for _ in range(10))
    for l in range(DEPTH):
        lw = dict(w_in=w_in[l], conv_a_w=conv_a_w[l], conv_b_w=conv_b_w[l], conv_b_b=conv_b_b[l],
                  norm_b_g=norm_b_g[l], norm_b_b=norm_b_b[l], pool_w=pool_w[l],
                  pool_scale=pool_scale[l], w_out=w_out[l], ln_g=ln_g[l], ln_b=ln_b[l],
                  sb_bias=sb_bias[l])
        hp, sa, sb, sc, kp, vp = _layer(hp, zero_a, zero_b, zero_c, _stick_breaking_prompt, 0, **lw)
        k_past = cache_k[l][page_table].reshape(bs, -1, N_HEADS_D, HEAD_DIM)
        v_past = cache_v[l][page_table].reshape(bs, -1, N_HEADS_D, HEAD_DIM)
        attend = functools.partial(_stick_breaking_sample, k_past=k_past, v_past=v_past)
        hs, ta, tb, tc, kn, vn = _layer(hs, state_conv_a[l], state_conv_b[l], state_pool[l], attend,
                                        k_past.shape[1], **lw)
        ca_p.append(sa); ca_s.append(ta); cb_p.append(sb); cb_s.append(tb)
        pl_p.append(sc); pl_s.append(tc); kp_l.append(kp); vp_l.append(vp)
        ks_l.append(kn); vs_l.append(vn)
    return (hp, hs,
            jnp.stack(ca_p), jnp.stack(ca_s),
            jnp.stack(cb_p), jnp.stack(cb_s),
            jnp.stack(pl_p), jnp.stack(pl_s),
            jnp.stack(kp_l), jnp.stack(vp_l),
            jnp.stack(ks_l), jnp.stack(vs_l))
```

```python
import functools

import jax
import jax.numpy as jnp
from jax import lax
from jax.experimental import pallas as pl
from jax.experimental.pallas import tpu as pltpu

F32 = jnp.float32
BF16 = jnp.bfloat16

N_HEADS = 4
CONV_A_WIDTH = 3
CONV_B_WIDTH = 31
POOL_WINDOWS = (2, 4, 8, 16)
POOL_HIST = max(POOL_WINDOWS) - 1
LN_EPS = 1e-5

LANES = 128
SUBLANES = 8
KEY_SUB = LANES
MIX_ROWS = 512
ATTN_ROWS = 256
ATTN_KEYS = 2 * KEY_SUB
SAMPLE_Q_ROWS = 16
VMEM_LIMIT = 48 * 1024 * 1024


def _sigmoid(x):
    return 1.0 / (1.0 + jnp.exp(-x))


def _silu(x):
    return x * _sigmoid(x)


def _softplus(z):
    return jnp.maximum(z, 0.0) + jnp.log(1.0 + jnp.exp(-jnp.abs(z)))


def _layernorm(x, g, b):
    mu = jnp.mean(x, axis=-1, keepdims=True)
    xc = x - mu
    var = jnp.mean(xc * xc, axis=-1, keepdims=True)
    return xc * lax.rsqrt(var + LN_EPS) * g + b


def _dot(a, b):
    return jnp.dot(a, b, preferred_element_type=F32)


def _dot_nt(a, b):
    return lax.dot_general(a, b, (((1,), (1,)), ((), ())), preferred_element_type=F32)


def _split_hi_lo(x):
    hi = x.astype(BF16)
    lo = (x - hi.astype(F32)).astype(BF16)
    return hi, lo


def _pool_lane_select(lane, group, vals):
    out = vals[-1]
    for g in range(len(vals) - 2, -1, -1):
        out = jnp.where(lane < (g + 1) * group, vals[g], out)
    return out


A_PAD = SUBLANES
B_PAD = 4 * SUBLANES
C_PAD = 2 * SUBLANES


def _prompt_mix_kernel(x_ref, w_ref, caw_ref, cbw_ref, cbb_ref, nbg_ref, nbb_ref, pbd_ref, psc_ref,
                       mix_ref, q_ref, kb_ref, vb_ref, kf_ref, vf_ref, gate_ref,
                       sa_ref, sb_ref, sc_ref, abuf, bbuf, cbuf, *, ts, w):
    s = pl.program_id(1)

    @pl.when(s == 0)
    def _():
        abuf[0:A_PAD] = jnp.zeros((A_PAD, w), F32)
        bbuf[0:B_PAD] = jnp.zeros((B_PAD, w), F32)
        cbuf[0:C_PAD] = jnp.zeros((C_PAD, w), F32)

    xb = x_ref[0].astype(BF16)

    ua = _dot(xb, w_ref[:, 0:4 * w])
    a_h, a_b, a_c, a_g = (ua[:, i * w:(i + 1) * w] for i in range(4))
    ach = a_c * a_h
    abuf[A_PAD:A_PAD + ts] = ach
    conv_a = caw_ref[2:3, :] * ach
    for i in range(CONV_A_WIDTH - 1):
        off = A_PAD - (CONV_A_WIDTH - 1) + i
        conv_a = conv_a + caw_ref[i:i + 1, :] * abuf[off:off + ts]
    mix_ref[0, :, 0:w] = (a_b * conv_a * _silu(a_g)).astype(BF16)
    sa_ref[0] = abuf[A_PAD + ts - (CONV_A_WIDTH - 1):A_PAD + ts]
    abuf[0:A_PAD] = abuf[ts:ts + A_PAD]

    ub = _dot(xb, w_ref[:, 4 * w:7 * w])
    b_a, b_b, b_g = (ub[:, i * w:(i + 1) * w] for i in range(3))
    bbuf[B_PAD:B_PAD + ts] = b_a * _sigmoid(b_b)
    conv_b = jnp.broadcast_to(cbb_ref[...], (ts, w))
    for i in range(CONV_B_WIDTH):
        off = B_PAD - (CONV_B_WIDTH - 1) + i
        conv_b = conv_b + cbw_ref[i:i + 1, :] * bbuf[off:off + ts]
    ln_b = _layernorm(conv_b, nbg_ref[...], nbb_ref[...])
    mix_ref[0, :, w:2 * w] = (_silu(ln_b) * _silu(b_g)).astype(BF16)
    sb_ref[0] = bbuf[B_PAD + ts - (CONV_B_WIDTH - 1):B_PAD + ts]
    bbuf[0:B_PAD] = bbuf[ts:ts + B_PAD]

    uc = _dot(xb, w_ref[:, 7 * w:9 * w])
    c_v, c_g = uc[:, 0:w], uc[:, w:2 * w]
    cbuf[C_PAD:C_PAD + ts] = c_v
    sums = []
    run = c_v
    back = 1
    for win in POOL_WINDOWS:
        while back < win:
            run = run + cbuf[C_PAD - back:C_PAD - back + ts]
            back += 1
        sums.append(run)
    lane = lax.broadcasted_iota(jnp.int32, (ts, w), 1)
    row = lax.broadcasted_iota(jnp.int32, (ts, w), 0)
    group = w // len(POOL_WINDOWS)
    win_sum = _pool_lane_select(lane, group, sums)
    win_len = _pool_lane_select(lane, group, [jnp.full((ts, w), v, jnp.int32) for v in POOL_WINDOWS])
    cnt = jnp.minimum(s * ts + row + 1, win_len).astype(F32)
    pooled = win_sum / cnt
    y_c = _dot((pooled - c_v).astype(BF16), pbd_ref[...]) * psc_ref[...]
    mix_ref[0, :, 2 * w:3 * w] = (y_c * _silu(c_g)).astype(BF16)
    sc_ref[0] = cbuf[C_PAD + ts - POOL_HIST:C_PAD + ts]
    cbuf[0:C_PAD] = cbuf[ts:ts + C_PAD]

    ud = _dot(xb, w_ref[:, 9 * w:13 * w])
    d_q, d_k, d_v, d_g = (ud[:, i * w:(i + 1) * w] for i in range(4))
    head_dim = w // N_HEADS
    q_ref[0] = (d_q * (head_dim ** -0.5)).astype(BF16)
    kb_ref[0] = d_k.astype(BF16)
    vb_ref[0] = d_v.astype(BF16)
    kf_ref[0] = d_k
    vf_ref[0] = d_v
    gate_ref[0] = _silu(d_g)


def _prompt_mix_call(x, w_in_b, caw, cbw, cbb, nbg, nbb, pbd, psc):
    b, l, d = x.shape
    d_in = w_in_b.shape[1]
    w = d_in // 13
    ts = min(MIX_ROWS, l)
    assert l % ts == 0 and ts % B_PAD == 0
    grid = (b, l // ts)
    row_spec = lambda width: pl.BlockSpec((1, ts, width), lambda i, j: (i, j, 0))
    full2 = lambda a: pl.BlockSpec(a.shape, lambda i, j: (0, 0))
    state_spec = lambda rows: pl.BlockSpec((1, rows, w), lambda i, j: (i, 0, 0))
    out_shape = (
        jax.ShapeDtypeStruct((b, l, 3 * w), BF16),
        jax.ShapeDtypeStruct((b, l, w), BF16),
        jax.ShapeDtypeStruct((b, l, w), BF16),
        jax.ShapeDtypeStruct((b, l, w), BF16),
        jax.ShapeDtypeStruct((b, l, w), F32),
        jax.ShapeDtypeStruct((b, l, w), F32),
        jax.ShapeDtypeStruct((b, l, w), F32),
        jax.ShapeDtypeStruct((b, CONV_A_WIDTH - 1, w), F32),
        jax.ShapeDtypeStruct((b, CONV_B_WIDTH - 1, w), F32),
        jax.ShapeDtypeStruct((b, POOL_HIST, w), F32),
    )
    out_specs = (row_spec(3 * w), row_spec(w), row_spec(w), row_spec(w), row_spec(w), row_spec(w),
                 row_spec(w), state_spec(CONV_A_WIDTH - 1), state_spec(CONV_B_WIDTH - 1),
                 state_spec(POOL_HIST))
    return pl.pallas_call(
        functools.partial(_prompt_mix_kernel, ts=ts, w=w),
        grid=grid,
        in_specs=[row_spec(d), full2(w_in_b), full2(caw), full2(cbw), full2(cbb), full2(nbg),
                  full2(nbb), full2(pbd), full2(psc)],
        out_specs=out_specs,
        out_shape=out_shape,
        scratch_shapes=[pltpu.VMEM((A_PAD + ts, w), F32), pltpu.VMEM((B_PAD + ts, w), F32),
                        pltpu.VMEM((C_PAD + ts, w), F32)],
        compiler_params=pltpu.CompilerParams(dimension_semantics=("arbitrary", "arbitrary"),
                                             vmem_limit_bytes=VMEM_LIMIT),
        name="prompt_mix",
    )(x, w_in_b, caw, cbw, cbb, nbg, nbb, pbd, psc)


def _stick_weights(z, carry, w_ur, mask):
    sp = _softplus(z)
    if mask is not None:
        sp = jnp.where(mask, sp, 0.0)
    hi, lo = _split_hi_lo(sp)
    zs = z - sp
    r1 = _dot(jnp.concatenate([hi[:, KEY_SUB:], lo[:, KEY_SUB:]], axis=1), w_ur)
    r0 = _dot(jnp.concatenate([hi[:, :KEY_SUB], lo[:, :KEY_SUB]], axis=1), w_ur)
    carry1 = carry + r1[:, KEY_SUB:]
    arg1 = zs[:, KEY_SUB:] - r1[:, :KEY_SUB] - carry
    arg0 = zs[:, :KEY_SUB] - r0[:, :KEY_SUB] - carry1
    a = jnp.exp(jnp.concatenate([arg0, arg1], axis=1))
    if mask is not None:
        a = jnp.where(mask, a, 0.0)
    return a, carry1 + r0[:, KEY_SUB:]


def _prompt_attn_kernel(bias_ref, q_ref, k_ref, v_ref, hm_ref, wur_ref, mix_ref, gate_ref, x_ref,
                        wout_ref, lng_ref, lnb_ref, y_ref, q4_scr, acc_scr, carry_scr,
                        *, tq, tk, alpha):
    i = pl.program_id(1)
    q = q_ref[0]
    for h in range(N_HEADS):
        q4_scr[h * tq:(h + 1) * tq] = jnp.where(hm_ref[h] > 0, q, jnp.zeros_like(q))
    acc_scr[...] = jnp.zeros_like(acc_scr)
    carry_scr[...] = jnp.zeros_like(carry_scr)
    w_ur = wur_ref[...]
    rows = lax.broadcasted_iota(jnp.int32, (tq, tk), 0)
    cols = lax.broadcasted_iota(jnp.int32, (tq, tk), 1)
    causal = cols < rows

    def block(j, mask):
        start = pl.multiple_of(j * tk, tk)
        k_blk = k_ref[0, pl.ds(start, tk), :]
        v_blk = v_ref[0, pl.ds(start, tk), :]
        z4 = _dot_nt(q4_scr[...], k_blk)
        a_parts, v_parts = [], []
        for h in range(N_HEADS):
            z = z4[h * tq:(h + 1) * tq] + bias_ref[h]
            a, carry = _stick_weights(z, carry_scr[h], w_ur, mask)
            carry_scr[h] = carry
            a_parts.append(a.astype(BF16))
            v_parts.append(jnp.where(hm_ref[h] > 0, v_blk, jnp.zeros_like(v_blk)))
        acc_scr[...] += _dot(jnp.concatenate(a_parts, axis=1), jnp.concatenate(v_parts, axis=0))

    block(i, causal)

    def body(t, c):
        block(i - 1 - t, None)
        return c

    lax.fori_loop(0, i, body, 0)

    y_d = (acc_scr[...] * gate_ref[0]).astype(BF16)
    mix = jnp.concatenate([mix_ref[0], y_d], axis=1)
    h_res = alpha * x_ref[0] + _dot(mix, wout_ref[...])
    y_ref[0] = _layernorm(h_res, lng_ref[...], lnb_ref[...])


def _prompt_attn_call(bias, q, kb, vb, hm, w_ur, mix, gate, x, w_out_b, lng, lnb, alpha):
    b, l, d = x.shape
    w = q.shape[-1]
    tq = tk = min(ATTN_ROWS, l)
    assert tk == ATTN_KEYS and l % tq == 0
    grid = (b, l // tq)
    row_spec = lambda width: pl.BlockSpec((1, tq, width), lambda i, j: (i, j, 0))
    seq_spec = pl.BlockSpec((1, l, w), lambda i, j: (i, 0, 0))
    full = lambda a: pl.BlockSpec(a.shape, lambda i, j: (0,) * a.ndim)
    return pl.pallas_call(
        functools.partial(_prompt_attn_kernel, tq=tq, tk=tk, alpha=alpha),
        grid=grid,
        in_specs=[pl.BlockSpec(memory_space=pltpu.SMEM), row_spec(w), seq_spec, seq_spec, full(hm),
                  full(w_ur), row_spec(3 * w), row_spec(w), row_spec(d), full(w_out_b), full(lng),
                  full(lnb)],
        out_specs=row_spec(d),
        out_shape=jax.ShapeDtypeStruct((b, l, d), F32),
        scratch_shapes=[pltpu.VMEM((N_HEADS * tq, w), BF16), pltpu.VMEM((tq, w), F32),
                        pltpu.VMEM((N_HEADS, tq, KEY_SUB), F32)],
        compiler_params=pltpu.CompilerParams(dimension_semantics=("arbitrary", "arbitrary"),
                                             vmem_limit_bytes=VMEM_LIMIT),
        name="prompt_attn",
    )(bias, q, kb, vb, hm, w_ur, mix, gate, x, w_out_b, lng, lnb)


def _sample_mix_kernel(x_ref, w_ref, caw_ref, cbw_ref, cbb_ref, nbg_ref, nbb_ref, pbd_ref, psc_ref,
                       sa_ref, sb_ref, sc_ref,
                       mix_ref, q_ref, k_ref, v_ref, gate_ref, sa_out, sb_out, sc_out, *, w, pos):
    xb = x_ref[...].astype(BF16)
    n = xb.shape[0]
    u = _dot(xb, w_ref[...])
    (a_h, a_b, a_c, a_g, b_a, b_b, b_g, c_v, c_g, d_q, d_k, d_v, d_g) = (
        u[:, i * w:(i + 1) * w] for i in range(13))

    ach = a_c * a_h
    conv_a = caw_ref[CONV_A_WIDTH - 1:CONV_A_WIDTH, :] * ach
    for i in range(CONV_A_WIDTH - 1):
        conv_a = conv_a + caw_ref[i:i + 1, :] * sa_ref[:, i * w:(i + 1) * w]
    mix_ref[:, 0:w] = (a_b * conv_a * _silu(a_g)).astype(BF16)
    for i in range(CONV_A_WIDTH - 2):
        sa_out[:, i * w:(i + 1) * w] = sa_ref[:, (i + 1) * w:(i + 2) * w]
    sa_out[:, (CONV_A_WIDTH - 2) * w:(CONV_A_WIDTH - 1) * w] = ach

    glu = b_a * _sigmoid(b_b)
    conv_b = cbb_ref[...] + cbw_ref[CONV_B_WIDTH - 1:CONV_B_WIDTH, :] * glu
    for i in range(CONV_B_WIDTH - 1):
        conv_b = conv_b + cbw_ref[i:i + 1, :] * sb_ref[:, i * w:(i + 1) * w]
    ln_b = _layernorm(conv_b, nbg_ref[...], nbb_ref[...])
    mix_ref[:, w:2 * w] = (_silu(ln_b) * _silu(b_g)).astype(BF16)
    sb_out[:, 0:(CONV_B_WIDTH - 2) * w] = sb_ref[:, w:(CONV_B_WIDTH - 1) * w]
    sb_out[:, (CONV_B_WIDTH - 2) * w:(CONV_B_WIDTH - 1) * w] = glu

    sums = []
    run = c_v
    back = 1
    for win in POOL_WINDOWS:
        while back < win:
            run = run + sc_ref[:, (POOL_HIST - back) * w:(POOL_HIST - back + 1) * w]
            back += 1
        sums.append(run)
    lane = lax.broadcasted_iota(jnp.int32, (n, w), 1)
    group = w // len(POOL_WINDOWS)
    win_sum = _pool_lane_select(lane, group, sums)
    cnt = _pool_lane_select(lane, group,
                            [jnp.full((n, w), float(min(pos + 1, v)), F32) for v in POOL_WINDOWS])
    pooled = win_sum / cnt
    y_c = _dot((pooled - c_v).astype(BF16), pbd_ref[...]) * psc_ref[...]
    mix_ref[:, 2 * w:3 * w] = (y_c * _silu(c_g)).astype(BF16)
    sc_out[:, 0:(POOL_HIST - 1) * w] = sc_ref[:, w:POOL_HIST * w]
    sc_out[:, (POOL_HIST - 1) * w:POOL_HIST * w] = c_v

    head_dim = w // N_HEADS
    q_ref[...] = d_q * (head_dim ** -0.5)
    k_ref[...] = d_k
    v_ref[...] = d_v
    gate_ref[...] = _silu(d_g)


def _sample_mix_call(x, w_in_b, caw, cbw, cbb, nbg, nbb, pbd, psc, sa, sb, sc, pos):
    n, d = x.shape
    w = w_in_b.shape[1] // 13
    out_shape = (
        jax.ShapeDtypeStruct((n, 3 * w), BF16),
        jax.ShapeDtypeStruct((n, w), F32),
        jax.ShapeDtypeStruct((n, w), F32),
        jax.ShapeDtypeStruct((n, w), F32),
        jax.ShapeDtypeStruct((n, w), F32),
        jax.ShapeDtypeStruct(sa.shape, F32),
        jax.ShapeDtypeStruct(sb.shape, F32),
        jax.ShapeDtypeStruct(sc.shape, F32),
    )
    return pl.pallas_call(
        functools.partial(_sample_mix_kernel, w=w, pos=pos),
        out_shape=out_shape,
        compiler_params=pltpu.CompilerParams(vmem_limit_bytes=VMEM_LIMIT),
        name="sample_mix",
    )(x, w_in_b, caw, cbw, cbb, nbg, nbb, pbd, psc, sa, sb, sc)


def _sample_attn_kernel(pt_ref, q_ref, hm_ref, bias_ref, wur_ref, *rest, n_pages):
    del pt_ref
    k_refs = rest[:n_pages]
    v_refs = rest[n_pages:2 * n_pages]
    o_ref = rest[2 * n_pages]
    hm = hm_ref[...]
    qbd = (jnp.broadcast_to(q_ref[0], hm.shape) * hm).astype(BF16)
    w_ur = wur_ref[...]
    bias = bias_ref[...]
    carry = jnp.zeros_like(bias)
    acc = jnp.zeros(hm.shape, F32)
    for p in range(n_pages - 1, -1, -1):
        z = _dot_nt(qbd, k_refs[p][...].astype(BF16)) + bias
        sp = _softplus(z)
        hi, lo = _split_hi_lo(sp)
        r = _dot(jnp.concatenate([hi, lo], axis=1), w_ur)
        a = jnp.exp(z - sp - r[:, :KEY_SUB] - carry)
        carry = carry + r[:, KEY_SUB:]
        acc = acc + _dot(a.astype(BF16), v_refs[p][...].astype(BF16))
    o_ref[0] = jnp.sum(acc * hm, axis=0, keepdims=True)


def _sample_attn_call(page_table, q, hm_rows, bias_rows, w_ur, cache_k, cache_v, layer):
    n, w = q.shape
    n_pages = page_table.shape[1]
    page = cache_k.shape[2]
    assert page == KEY_SUB
    q3 = q.reshape(n, 1, w)

    def page_spec(p):
        return pl.BlockSpec((None, None, page, w), lambda i, pt: (layer, pt[i, p], 0, 0))

    full = lambda a: pl.BlockSpec(a.shape, lambda i, pt: (0,) * a.ndim)
    grid_spec = pltpu.PrefetchScalarGridSpec(
        num_scalar_prefetch=1,
        grid=(n,),
        in_specs=[pl.BlockSpec((1, 1, w), lambda i, pt: (i, 0, 0)), full(hm_rows), full(bias_rows),
                  full(w_ur)] + [page_spec(p) for p in range(n_pages)] * 2,
        out_specs=pl.BlockSpec((1, 1, w), lambda i, pt: (i, 0, 0)),
    )
    out = pl.pallas_call(
        functools.partial(_sample_attn_kernel, n_pages=n_pages),
        grid_spec=grid_spec,
        out_shape=jax.ShapeDtypeStruct((n, 1, w), F32),
        compiler_params=pltpu.CompilerParams(dimension_semantics=("arbitrary",),
                                             vmem_limit_bytes=VMEM_LIMIT),
        name="sample_attn",
    )(page_table, q3, hm_rows, bias_rows, w_ur, *([cache_k] * n_pages), *([cache_v] * n_pages))
    return out.reshape(n, w)


def _sample_out_kernel(mix_ref, attn_ref, gate_ref, x_ref, wout_ref, lng_ref, lnb_ref, y_ref, *, alpha):
    y_d = (attn_ref[...] * gate_ref[...]).astype(BF16)
    mix = jnp.concatenate([mix_ref[...], y_d], axis=1)
    h_res = alpha * x_ref[...] + _dot(mix, wout_ref[...])
    y_ref[...] = _layernorm(h_res, lng_ref[...], lnb_ref[...])


def _sample_out_call(mix, attn, gate, x, w_out_b, lng, lnb, alpha):
    return pl.pallas_call(
        functools.partial(_sample_out_kernel, alpha=alpha),
        out_shape=jax.ShapeDtypeStruct(x.shape, F32),
        compiler_params=pltpu.CompilerParams(vmem_limit_bytes=VMEM_LIMIT),
        name="sample_out",
    )(mix, attn, gate, x, w_out_b, lng, lnb)


def _head_mask(rows, w):
    head = jnp.arange(w, dtype=jnp.int32) // (w // N_HEADS)
    return (head[None, None, :] == jnp.arange(N_HEADS, dtype=jnp.int32)[:, None, None]) & jnp.ones(
        (1, rows, 1), bool)


def _cumsum_matrix():
    j = jnp.arange(KEY_SUB)[:, None]
    s = jnp.arange(KEY_SUB)[None, :]
    upper = (j > s).astype(BF16)
    half = jnp.concatenate([upper, jnp.ones((KEY_SUB, KEY_SUB), BF16)], axis=1)
    return jnp.concatenate([half, half], axis=0)


def kernel(x_prompt, x_sample, cache_k, cache_v, page_table, state_conv_a, state_conv_b, state_pool,
           w_in, conv_a_w, conv_b_w, conv_b_b, norm_b_g, norm_b_b, pool_w, pool_scale, w_out,
           ln_g, ln_b, sb_bias):
    depth = w_in.shape[0]
    bp, seq, d_model = x_prompt.shape
    bs = x_sample.shape[0]
    w = w_in.shape[2] // 13
    head_dim = w // N_HEADS
    n_pool, page = cache_k.shape[1], cache_k.shape[2]
    past_len = page_table.shape[1] * page
    alpha = (2 * depth) ** 0.25

    w_ur = _cumsum_matrix()
    hm_attn = _head_mask(min(ATTN_ROWS, seq), w).astype(BF16)
    hm_rows = jnp.pad(_head_mask(1, w)[:, 0, :].astype(F32),
                      ((0, SAMPLE_Q_ROWS - N_HEADS), (0, 0)))
    cache_k4 = cache_k.reshape(depth, n_pool, page, w)
    cache_v4 = cache_v.reshape(depth, n_pool, page, w)

    hp = x_prompt
    hs = x_sample.reshape(bs, d_model)
    outs = [[] for _ in range(10)]
    for l in range(depth):
        w_in_b = w_in[l].astype(BF16)
        w_out_b = w_out[l].astype(BF16)
        pbd = jax.scipy.linalg.block_diag(*[pool_w[l, g] for g in range(pool_w.shape[1])]).astype(BF16)
        row = lambda a: a[l].reshape(1, -1)
        small = (conv_a_w[l], conv_b_w[l], row(conv_b_b), row(norm_b_g), row(norm_b_b), pbd,
                 row(pool_scale))
        lng, lnb = row(ln_g), row(ln_b)

        mix, q, kb, vb, kf, vf, gate, sa, sb, sc = _prompt_mix_call(hp, w_in_b, *small)
        hp = _prompt_attn_call(sb_bias[l], q, kb, vb, hm_attn, w_ur, mix, gate, hp, w_out_b, lng, lnb,
                               alpha)

        s_mix, s_q, s_k, s_v, s_gate, ta, tb, tc = _sample_mix_call(
            hs, w_in_b, *small, state_conv_a[l].reshape(bs, -1), state_conv_b[l].reshape(bs, -1),
            state_pool[l].reshape(bs, -1), past_len)
        bias_rows = jnp.broadcast_to(
            jnp.pad(sb_bias[l], (0, SAMPLE_Q_ROWS - N_HEADS))[:, None], (SAMPLE_Q_ROWS, KEY_SUB))
        s_attn = _sample_attn_call(page_table, s_q, hm_rows, bias_rows, w_ur, cache_k4, cache_v4, l)
        hs = _sample_out_call(s_mix, s_attn, s_gate, hs, w_out_b, lng, lnb, alpha)

        for lst, val in zip(outs, (
                sa, ta.reshape(bs, CONV_A_WIDTH - 1, w), sb, tb.reshape(bs, CONV_B_WIDTH - 1, w),
                sc, tc.reshape(bs, POOL_HIST, w),
                kf.reshape(bp, seq, N_HEADS, head_dim), vf.reshape(bp, seq, N_HEADS, head_dim),
                s_k.reshape(bs, 1, N_HEADS, head_dim), s_v.reshape(bs, 1, N_HEADS, head_dim))):
            lst.append(val)

    return (hp, hs.reshape(bs, 1, d_model)) + tuple(jnp.stack(o) for o in outs)
```

```python
import functools
import math

import jax
import jax.numpy as jnp
from jax import lax
from jax.experimental import pallas as pl
from jax.experimental.pallas import tpu as pltpu

F32 = jnp.float32
BF16 = jnp.bfloat16

N_HEADS = 4
CONV_A_WIDTH = 3
CONV_B_WIDTH = 31
POOL_WINDOWS = (2, 4, 8, 16)
POOL_HIST = max(POOL_WINDOWS) - 1
LN_EPS = 1e-5
LOG2E = math.log2(math.e)
EXP2_CLAMP = 126.0

LANES = 128
SUBLANES = 8
KEY_SUB = LANES
MIX_ROWS = 512
ATTN_ROWS = 256
ATTN_KEYS = 2 * KEY_SUB
VMEM_LIMIT = 48 * 1024 * 1024


def _sigmoid(x):
    return 1.0 / (1.0 + jnp.exp(-x))


def _silu(x):
    return x * _sigmoid(x)


def _softplus2(u):
    return jnp.maximum(u, jnp.log(1.0 + jnp.exp2(jnp.minimum(u, EXP2_CLAMP))) * LOG2E)


def _layernorm(x, g, b):
    mu = jnp.mean(x, axis=-1, keepdims=True)
    xc = x - mu
    var = jnp.mean(xc * xc, axis=-1, keepdims=True)
    return xc * lax.rsqrt(var + LN_EPS) * g + b


def _dot(a, b):
    return jnp.dot(a, b, preferred_element_type=F32)


def _split_hi_lo(x):
    bits = lax.bitcast_convert_type(x, jnp.uint32) & jnp.uint32(0xFFFF0000)
    hi = lax.bitcast_convert_type(bits, F32)
    return hi.astype(BF16), (x - hi).astype(BF16)


def _pool_lane_select(lane, group, vals):
    out = vals[-1]
    for g in range(len(vals) - 2, -1, -1):
        out = jnp.where(lane < (g + 1) * group, vals[g], out)
    return out


A_PAD = SUBLANES
B_PAD = 4 * SUBLANES
C_PAD = 2 * SUBLANES


def _prompt_mix_kernel(x_ref, w_ref, caw_ref, cbw_ref, cbb_ref, nbg_ref, nbb_ref, pbd_ref, psc_ref,
                       mix_ref, q_ref, ktb_ref, vb_ref, ktf_ref, vtf_ref, gate_ref,
                       sa_ref, sb_ref, sc_ref, abuf, bbuf, cbuf, *, ts, w, q_scale):
    s = pl.program_id(1)

    @pl.when(s == 0)
    def _():
        abuf[0:A_PAD] = jnp.zeros((A_PAD, w), F32)
        bbuf[0:B_PAD] = jnp.zeros((B_PAD, w), F32)
        cbuf[0:C_PAD] = jnp.zeros((C_PAD, w), F32)

    xb = x_ref[0].astype(BF16)

    ua = _dot(xb, w_ref[:, 0:4 * w])
    a_h, a_b, a_c, a_g = (ua[:, i * w:(i + 1) * w] for i in range(4))
    ach = a_c * a_h
    abuf[A_PAD:A_PAD + ts] = ach
    conv_a = caw_ref[2:3, :] * ach
    for i in range(CONV_A_WIDTH - 1):
        off = A_PAD - (CONV_A_WIDTH - 1) + i
        conv_a = conv_a + caw_ref[i:i + 1, :] * abuf[off:off + ts]
    mix_ref[0, :, 0:w] = (a_b * conv_a * _silu(a_g)).astype(BF16)
    sa_ref[0] = abuf[A_PAD + ts - (CONV_A_WIDTH - 1):A_PAD + ts]
    abuf[0:A_PAD] = abuf[ts:ts + A_PAD]

    ub = _dot(xb, w_ref[:, 4 * w:7 * w])
    b_a, b_b, b_g = (ub[:, i * w:(i + 1) * w] for i in range(3))
    bbuf[B_PAD:B_PAD + ts] = b_a * _sigmoid(b_b)
    conv_b = jnp.broadcast_to(cbb_ref[...], (ts, w))
    for i in range(CONV_B_WIDTH):
        off = B_PAD - (CONV_B_WIDTH - 1) + i
        conv_b = conv_b + cbw_ref[i:i + 1, :] * bbuf[off:off + ts]
    ln_b = _layernorm(conv_b, nbg_ref[...], nbb_ref[...])
    mix_ref[0, :, w:2 * w] = (_silu(ln_b) * _silu(b_g)).astype(BF16)
    sb_ref[0] = bbuf[B_PAD + ts - (CONV_B_WIDTH - 1):B_PAD + ts]
    bbuf[0:B_PAD] = bbuf[ts:ts + B_PAD]

    uc = _dot(xb, w_ref[:, 7 * w:9 * w])
    c_v, c_g = uc[:, 0:w], uc[:, w:2 * w]
    cbuf[C_PAD:C_PAD + ts] = c_v
    sums = []
    run = c_v
    back = 1
    for win in POOL_WINDOWS:
        while back < win:
            run = run + cbuf[C_PAD - back:C_PAD - back + ts]
            back += 1
        sums.append(run)
    lane = lax.broadcasted_iota(jnp.int32, (ts, w), 1)
    row = lax.broadcasted_iota(jnp.int32, (ts, w), 0)
    group = w // len(POOL_WINDOWS)
    win_sum = _pool_lane_select(lane, group, sums)
    win_len = _pool_lane_select(lane, group, [jnp.full((ts, w), v, jnp.int32) for v in POOL_WINDOWS])
    cnt = jnp.minimum(s * ts + row + 1, win_len).astype(F32)
    pooled = win_sum / cnt
    y_c = _dot((pooled - c_v).astype(BF16), pbd_ref[...]) * psc_ref[...]
    mix_ref[0, :, 2 * w:3 * w] = (y_c * _silu(c_g)).astype(BF16)
    sc_ref[0] = cbuf[C_PAD + ts - POOL_HIST:C_PAD + ts]
    cbuf[0:C_PAD] = cbuf[ts:ts + C_PAD]

    ud = _dot(xb, w_ref[:, 9 * w:13 * w])
    d_q, d_k, d_v, d_g = (ud[:, i * w:(i + 1) * w] for i in range(4))
    q_ref[0] = (d_q * q_scale).astype(BF16)
    k_t = jnp.transpose(d_k)
    ktf_ref[0] = k_t
    ktb_ref[0] = k_t.astype(BF16)
    vtf_ref[0] = jnp.transpose(d_v)
    vb_ref[0] = d_v.astype(BF16)
    gate_ref[0] = _silu(d_g)


def _prompt_mix_call(x, w_in_b, caw, cbw, cbb, nbg, nbb, pbd, psc, q_scale):
    b, l, d = x.shape
    d_in = w_in_b.shape[1]
    w = d_in // 13
    ts = min(MIX_ROWS, l)
    assert l % ts == 0 and ts % B_PAD == 0
    grid = (b, l // ts)
    row_spec = lambda width: pl.BlockSpec((1, ts, width), lambda i, j: (i, j, 0))
    col_spec = pl.BlockSpec((1, w, ts), lambda i, j: (i, 0, j))
    full2 = lambda a: pl.BlockSpec(a.shape, lambda i, j: (0, 0))
    state_spec = lambda rows: pl.BlockSpec((1, rows, w), lambda i, j: (i, 0, 0))
    out_shape = (
        jax.ShapeDtypeStruct((b, l, 3 * w), BF16),
        jax.ShapeDtypeStruct((b, l, w), BF16),
        jax.ShapeDtypeStruct((b, w, l), BF16),
        jax.ShapeDtypeStruct((b, l, w), BF16),
        jax.ShapeDtypeStruct((b, w, l), F32),
        jax.ShapeDtypeStruct((b, w, l), F32),
        jax.ShapeDtypeStruct((b, l, w), F32),
        jax.ShapeDtypeStruct((b, CONV_A_WIDTH - 1, w), F32),
        jax.ShapeDtypeStruct((b, CONV_B_WIDTH - 1, w), F32),
        jax.ShapeDtypeStruct((b, POOL_HIST, w), F32),
    )
    out_specs = (row_spec(3 * w), row_spec(w), col_spec, row_spec(w), col_spec, col_spec,
                 row_spec(w), state_spec(CONV_A_WIDTH - 1), state_spec(CONV_B_WIDTH - 1),
                 state_spec(POOL_HIST))
    return pl.pallas_call(
        functools.partial(_prompt_mix_kernel, ts=ts, w=w, q_scale=q_scale),
        grid=grid,
        in_specs=[row_spec(d), full2(w_in_b), full2(caw), full2(cbw), full2(cbb), full2(nbg),
                  full2(nbb), full2(pbd), full2(psc)],
        out_specs=out_specs,
        out_shape=out_shape,
        scratch_shapes=[pltpu.VMEM((A_PAD + ts, w), F32), pltpu.VMEM((B_PAD + ts, w), F32),
                        pltpu.VMEM((C_PAD + ts, w), F32)],
        compiler_params=pltpu.CompilerParams(dimension_semantics=("arbitrary", "arbitrary"),
                                             vmem_limit_bytes=VMEM_LIMIT),
        name="prompt_mix",
    )(x, w_in_b, caw, cbw, cbb, nbg, nbb, pbd, psc)


def _stick_weights(u, carry, w_ur, mask):
    sp = _softplus2(u)
    if mask is not None:
        sp = jnp.where(mask, sp, 0.0)
    hi, lo = _split_hi_lo(sp)
    r1 = _dot(jnp.concatenate([hi[:, KEY_SUB:], lo[:, KEY_SUB:]], axis=1), w_ur)
    r0 = _dot(jnp.concatenate([hi[:, :KEY_SUB], lo[:, :KEY_SUB]], axis=1), w_ur)
    carry1 = carry + r1[:, KEY_SUB:]
    arg1 = u[:, KEY_SUB:] - r1[:, :KEY_SUB] - carry
    arg0 = u[:, :KEY_SUB] - r0[:, :KEY_SUB] - carry1
    a = jnp.exp2(jnp.concatenate([arg0, arg1], axis=1))
    if mask is not None:
        a = jnp.where(mask, a, 0.0)
    return a, carry1 + r0[:, KEY_SUB:]


def _prompt_attn_kernel(bias_ref, q_ref, kt_ref, v_ref, hm_ref, wur_ref, mix_ref, gate_ref, x_ref,
                        wout_ref, lng_ref, lnb_ref, y_ref, q4_scr, acc_scr, carry_scr,
                        *, tq, tk, alpha):
    i = pl.program_id(1)
    q = q_ref[0]
    for h in range(N_HEADS):
        q4_scr[h * tq:(h + 1) * tq] = jnp.where(hm_ref[h] > 0, q, jnp.zeros_like(q))
    acc_scr[...] = jnp.zeros_like(acc_scr)
    carry_scr[...] = jnp.zeros_like(carry_scr)
    w_ur = wur_ref[...]
    rows = lax.broadcasted_iota(jnp.int32, (tq, tk), 0)
    cols = lax.broadcasted_iota(jnp.int32, (tq, tk), 1)
    causal = cols < rows

    def block(j, mask):
        start = pl.multiple_of(j * tk, tk)
        kt_blk = kt_ref[0, :, pl.ds(start, tk)]
        v_blk = v_ref[0, pl.ds(start, tk), :]
        u4 = _dot(q4_scr[...], kt_blk)
        a_parts, v_parts = [], []
        for h in range(N_HEADS):
            u = u4[h * tq:(h + 1) * tq] + bias_ref[h]
            a, carry = _stick_weights(u, carry_scr[h], w_ur, mask)
            carry_scr[h] = carry
            a_parts.append(a.astype(BF16))
            v_parts.append(jnp.where(hm_ref[h] > 0, v_blk, jnp.zeros_like(v_blk)))
        acc_scr[...] += _dot(jnp.concatenate(a_parts, axis=1), jnp.concatenate(v_parts, axis=0))

    block(i, causal)

    def body(t, c):
        block(i - 1 - t, None)
        return c

    lax.fori_loop(0, i, body, 0)

    y_d = (acc_scr[...] * gate_ref[0]).astype(BF16)
    mix = jnp.concatenate([mix_ref[0], y_d], axis=1)
    h_res = alpha * x_ref[0] + _dot(mix, wout_ref[...])
    y_ref[0] = _layernorm(h_res, lng_ref[...], lnb_ref[...])


def _prompt_attn_call(bias2, q, ktb, vb, hm, w_ur, mix, gate, x, w_out_b, lng, lnb, alpha):
    b, l, d = x.shape
    w = q.shape[-1]
    tq = tk = min(ATTN_ROWS, l)
    assert tk == ATTN_KEYS and l % tq == 0
    grid = (b, l // tq)
    row_spec = lambda width: pl.BlockSpec((1, tq, width), lambda i, j: (i, j, 0))
    full = lambda a: pl.BlockSpec(a.shape, lambda i, j: (0,) * a.ndim)
    return pl.pallas_call(
        functools.partial(_prompt_attn_kernel, tq=tq, tk=tk, alpha=alpha),
        grid=grid,
        in_specs=[pl.BlockSpec(memory_space=pltpu.SMEM), row_spec(w),
                  pl.BlockSpec((1, w, l), lambda i, j: (i, 0, 0)),
                  pl.BlockSpec((1, l, w), lambda i, j: (i, 0, 0)), full(hm),
                  full(w_ur), row_spec(3 * w), row_spec(w), row_spec(d), full(w_out_b), full(lng),
                  full(lnb)],
        out_specs=row_spec(d),
        out_shape=jax.ShapeDtypeStruct((b, l, d), F32),
        scratch_shapes=[pltpu.VMEM((N_HEADS * tq, w), BF16), pltpu.VMEM((tq, w), F32),
                        pltpu.VMEM((N_HEADS, tq, KEY_SUB), F32)],
        compiler_params=pltpu.CompilerParams(dimension_semantics=("arbitrary", "arbitrary"),
                                             vmem_limit_bytes=VMEM_LIMIT),
        name="prompt_attn",
    )(bias2, q, ktb, vb, hm, w_ur, mix, gate, x, w_out_b, lng, lnb)


def _sample_mix_kernel(x_ref, w_ref, caw_ref, cbw_ref, cbb_ref, nbg_ref, nbb_ref, pbd_ref, psc_ref,
                       sa_ref, sb_ref, sc_ref,
                       mix_ref, q_ref, k_ref, v_ref, gate_ref, sa_out, sb_out, sc_out,
                       *, w, pos, q_scale):
    xb = x_ref[...].astype(BF16)
    n = xb.shape[0]
    u = _dot(xb, w_ref[...])
    (a_h, a_b, a_c, a_g, b_a, b_b, b_g, c_v, c_g, d_q, d_k, d_v, d_g) = (
        u[:, i * w:(i + 1) * w] for i in range(13))

    ach = a_c * a_h
    conv_a = caw_ref[CONV_A_WIDTH - 1:CONV_A_WIDTH, :] * ach
    for i in range(CONV_A_WIDTH - 1):
        conv_a = conv_a + caw_ref[i:i + 1, :] * sa_ref[i]
    mix_ref[:, 0:w] = (a_b * conv_a * _silu(a_g)).astype(BF16)
    for i in range(CONV_A_WIDTH - 2):
        sa_out[i] = sa_ref[i + 1]
    sa_out[CONV_A_WIDTH - 2] = ach

    glu = b_a * _sigmoid(b_b)
    conv_b = cbb_ref[...] + cbw_ref[CONV_B_WIDTH - 1:CONV_B_WIDTH, :] * glu
    for i in range(CONV_B_WIDTH - 1):
        conv_b = conv_b + cbw_ref[i:i + 1, :] * sb_ref[i]
    ln_b = _layernorm(conv_b, nbg_ref[...], nbb_ref[...])
    mix_ref[:, w:2 * w] = (_silu(ln_b) * _silu(b_g)).astype(BF16)
    for i in range(CONV_B_WIDTH - 2):
        sb_out[i] = sb_ref[i + 1]
    sb_out[CONV_B_WIDTH - 2] = glu

    sums = []
    run = c_v
    back = 1
    for win in POOL_WINDOWS:
        while back < win:
            run = run + sc_ref[POOL_HIST - back]
            back += 1
        sums.append(run)
    lane = lax.broadcasted_iota(jnp.int32, (n, w), 1)
    group = w // len(POOL_WINDOWS)
    win_sum = _pool_lane_select(lane, group, sums)
    cnt = _pool_lane_select(lane, group,
                            [jnp.full((n, w), float(min(pos + 1, v)), F32) for v in POOL_WINDOWS])
    pooled = win_sum / cnt
    y_c = _dot((pooled - c_v).astype(BF16), pbd_ref[...]) * psc_ref[...]
    mix_ref[:, 2 * w:3 * w] = (y_c * _silu(c_g)).astype(BF16)
    for i in range(POOL_HIST - 1):
        sc_out[i] = sc_ref[i + 1]
    sc_out[POOL_HIST - 1] = c_v

    q_ref[...] = d_q * q_scale
    k_ref[...] = d_k
    v_ref[...] = d_v
    gate_ref[...] = _silu(d_g)


def _sample_mix_call(x, w_in_b, caw, cbw, cbb, nbg, nbb, pbd, psc, sa, sb, sc, pos, q_scale):
    n, d = x.shape
    w = w_in_b.shape[1] // 13
    out_shape = (
        jax.ShapeDtypeStruct((n, 3 * w), BF16),
        jax.ShapeDtypeStruct((n, w), F32),
        jax.ShapeDtypeStruct((n, w), F32),
        jax.ShapeDtypeStruct((n, w), F32),
        jax.ShapeDtypeStruct((n, w), F32),
        jax.ShapeDtypeStruct(sa.shape, F32),
        jax.ShapeDtypeStruct(sb.shape, F32),
        jax.ShapeDtypeStruct(sc.shape, F32),
    )
    return pl.pallas_call(
        functools.partial(_sample_mix_kernel, w=w, pos=pos, q_scale=q_scale),
        out_shape=out_shape,
        compiler_params=pltpu.CompilerParams(vmem_limit_bytes=VMEM_LIMIT),
        name="sample_mix",
    )(x, w_in_b, caw, cbw, cbb, nbg, nbb, pbd, psc, sa, sb, sc)


def _sample_attn_kernel(pt_ref, q_ref, bias_ref, wur_ref, tsuf_ref, *rest, n_pages, w):
    del pt_ref
    k_refs = rest[:n_pages]
    v_refs = rest[n_pages:2 * n_pages]
    o_ref = rest[2 * n_pages]
    u_scr = rest[2 * n_pages + 1]
    hd = w // N_HEADS
    page = k_refs[0].shape[1]
    qcol = jnp.transpose(jnp.broadcast_to(q_ref[0], (page, w)))
    for p in range(n_pages):
        prod = k_refs[p][...] * qcol
        u_scr[N_HEADS * p:N_HEADS * (p + 1), :] = jnp.sum(prod.reshape(N_HEADS, hd, page), axis=1)
    u = u_scr[...] + bias_ref[...]
    hi, lo = _split_hi_lo(_softplus2(u))
    r = _dot(jnp.concatenate([hi, lo], axis=1), wur_ref[...])
    s_hi, s_lo = _split_hi_lo(r[:, page:])
    carry = _dot(tsuf_ref[...], jnp.concatenate([s_hi, s_lo], axis=0))
    a = jnp.exp2(u - r[:, :page] - carry)
    acc = jnp.zeros((w, page), F32)
    for p in range(n_pages):
        a_rows = jnp.concatenate(
            [jnp.broadcast_to(a[N_HEADS * p + h:N_HEADS * p + h + 1, :], (hd, page))
             for h in range(N_HEADS)], axis=0)
        acc = acc + a_rows * v_refs[p][...]
    o_ref[0] = jnp.sum(jnp.transpose(acc), axis=0, keepdims=True)


def _sample_attn_call(page_table, q, bias_rows, w_ur, tsuf, cache_kt, cache_vt, layer):
    n, w = q.shape
    n_pages = page_table.shape[1]
    page = cache_kt.shape[3]
    assert page == KEY_SUB
    q3 = q.reshape(n, 1, w)

    def page_spec(p):
        return pl.BlockSpec((None, None, w, page), lambda i, pt: (layer, pt[i, p], 0, 0))

    full = lambda a: pl.BlockSpec(a.shape, lambda i, pt: (0,) * a.ndim)
    grid_spec = pltpu.PrefetchScalarGridSpec(
        num_scalar_prefetch=1,
        grid=(n,),
        in_specs=[pl.BlockSpec((1, 1, w), lambda i, pt: (i, 0, 0)), full(bias_rows), full(w_ur),
                  full(tsuf)] + [page_spec(p) for p in range(n_pages)] * 2,
        out_specs=pl.BlockSpec((1, 1, w), lambda i, pt: (i, 0, 0)),
        scratch_shapes=[pltpu.VMEM((N_HEADS * n_pages, page), F32)],
    )
    out = pl.pallas_call(
        functools.partial(_sample_attn_kernel, n_pages=n_pages, w=w),
        grid_spec=grid_spec,
        out_shape=jax.ShapeDtypeStruct((n, 1, w), F32),
        compiler_params=pltpu.CompilerParams(dimension_semantics=("arbitrary",),
                                             vmem_limit_bytes=VMEM_LIMIT),
        name="sample_attn",
    )(page_table, q3, bias_rows, w_ur, tsuf, *([cache_kt] * n_pages), *([cache_vt] * n_pages))
    return out.reshape(n, w)


def _sample_out_kernel(mix_ref, attn_ref, gate_ref, x_ref, wout_ref, lng_ref, lnb_ref, y_ref, *, alpha):
    y_d = (attn_ref[...] * gate_ref[...]).astype(BF16)
    mix = jnp.concatenate([mix_ref[...], y_d], axis=1)
    h_res = alpha * x_ref[...] + _dot(mix, wout_ref[...])
    y_ref[...] = _layernorm(h_res, lng_ref[...], lnb_ref[...])


def _sample_out_call(mix, attn, gate, x, w_out_b, lng, lnb, alpha):
    return pl.pallas_call(
        functools.partial(_sample_out_kernel, alpha=alpha),
        out_shape=jax.ShapeDtypeStruct(x.shape, F32),
        compiler_params=pltpu.CompilerParams(vmem_limit_bytes=VMEM_LIMIT),
        name="sample_out",
    )(mix, attn, gate, x, w_out_b, lng, lnb)


def _head_mask(rows, w):
    head = jnp.arange(w, dtype=jnp.int32) // (w // N_HEADS)
    return jnp.broadcast_to(
        head[None, None, :] == jnp.arange(N_HEADS, dtype=jnp.int32)[:, None, None], (N_HEADS, rows, w))


def _suffix_matrix():
    j = jnp.arange(KEY_SUB)[:, None]
    s = jnp.arange(KEY_SUB)[None, :]
    half = jnp.concatenate([(j >= s).astype(BF16), jnp.ones((KEY_SUB, KEY_SUB), BF16)], axis=1)
    return jnp.concatenate([half, half], axis=0)


def _page_suffix_matrix(n_pages):
    r = jnp.arange(N_HEADS * n_pages)
    t = ((r[:, None] % N_HEADS) == (r[None, :] % N_HEADS)) & ((r[None, :] // N_HEADS) > (r[:, None] // N_HEADS))
    t = t.astype(BF16)
    return jnp.concatenate([t, t], axis=1)


def kernel(x_prompt, x_sample, cache_k, cache_v, page_table, state_conv_a, state_conv_b, state_pool,
           w_in, conv_a_w, conv_b_w, conv_b_b, norm_b_g, norm_b_b, pool_w, pool_scale, w_out,
           ln_g, ln_b, sb_bias):
    depth = w_in.shape[0]
    bp, seq, d_model = x_prompt.shape
    bs = x_sample.shape[0]
    w = w_in.shape[2] // 13
    head_dim = w // N_HEADS
    n_pool, page = cache_k.shape[1], cache_k.shape[2]
    n_pages = page_table.shape[1]
    past_len = n_pages * page
    alpha = (2 * depth) ** 0.25
    q_scale = head_dim ** -0.5 * LOG2E

    w_ur = _suffix_matrix()
    tsuf = _page_suffix_matrix(n_pages)
    hm_attn = _head_mask(min(ATTN_ROWS, seq), w).astype(BF16)
    cache_kt = cache_k.transpose(0, 1, 3, 4, 2).reshape(depth, n_pool, w, page)
    cache_vt = cache_v.transpose(0, 1, 3, 4, 2).reshape(depth, n_pool, w, page)
    from_t = lambda a: a.reshape(bp, N_HEADS, head_dim, seq).transpose(0, 3, 1, 2)

    hp = x_prompt
    hs = x_sample.reshape(bs, d_model)
    outs = [[] for _ in range(10)]
    for l in range(depth):
        w_in_b = w_in[l].astype(BF16)
        w_out_b = w_out[l].astype(BF16)
        pbd = jax.scipy.linalg.block_diag(*[pool_w[l, g] for g in range(pool_w.shape[1])]).astype(BF16)
        row = lambda a: a[l].reshape(1, -1)
        small = (conv_a_w[l], conv_b_w[l], row(conv_b_b), row(norm_b_g), row(norm_b_b), pbd,
                 row(pool_scale))
        lng, lnb = row(ln_g), row(ln_b)
        bias2 = sb_bias[l] * LOG2E

        mix, q, ktb, vb, ktf, vtf, gate, sa, sb, sc = _prompt_mix_call(hp, w_in_b, *small, q_scale)
        hp = _prompt_attn_call(bias2, q, ktb, vb, hm_attn, w_ur, mix, gate, hp, w_out_b, lng, lnb,
                               alpha)

        s_mix, s_q, s_k, s_v, s_gate, ta, tb, tc = _sample_mix_call(
            hs, w_in_b, *small, state_conv_a[l].transpose(1, 0, 2), state_conv_b[l].transpose(1, 0, 2),
            state_pool[l].transpose(1, 0, 2), past_len, q_scale)
        bias_rows = jnp.broadcast_to(jnp.tile(bias2, n_pages)[:, None], (N_HEADS * n_pages, page))
        s_attn = _sample_attn_call(page_table, s_q, bias_rows, w_ur, tsuf, cache_kt, cache_vt, l)
        hs = _sample_out_call(s_mix, s_attn, s_gate, hs, w_out_b, lng, lnb, alpha)

        for lst, val in zip(outs, (
                sa, ta.transpose(1, 0, 2), sb, tb.transpose(1, 0, 2), sc, tc.transpose(1, 0, 2),
                from_t(ktf), from_t(vtf),
                s_k.reshape(bs, 1, N_HEADS, head_dim), s_v.reshape(bs, 1, N_HEADS, head_dim))):
            lst.append(val)

    return (hp, hs.reshape(bs, 1, d_model)) + tuple(jnp.stack(o) for o in outs)
```
